```python
import math
import jax, jax.numpy as jnp
from jax import lax
import numpy as np

D_MODEL = 2048
BATCH = 2
SEQ = 16384
DEPTH = 2

N_HEADS = 16
HEAD_DIM = D_MODEL // N_HEADS // 2
V_DIM = 2 * HEAD_DIM
ROT_DIM = HEAD_DIM // 4
ROPE_THETA = 500000.0
Q_BLOCK = 128
CONV_WIDTH = 31
CONV_DIM = D_MODEL
D_FF = 5632
N_EXPERTS = 8
TOP_K = 2
D_FF_EXPERT = 7168
EXPERT_BLOCK = 256
ALPHA = (2.0 * DEPTH) ** 0.25
BETA = (8.0 * DEPTH) ** -0.25
LN_EPS = 1e-5

kernel_name = "hybrid_diffattn_conformer_moe_deepnorm"


def layer_norm(x, g, b):
    xf = x.astype(jnp.float32)
    mu = jnp.mean(xf, axis=-1, keepdims=True)
    var = jnp.mean(jnp.square(xf - mu), axis=-1, keepdims=True)
    y = (xf - mu) * lax.rsqrt(var + LN_EPS) * g.astype(jnp.float32) + b.astype(jnp.float32)
    return y.astype(x.dtype)


def rms_norm(x, g):
    xf = x.astype(jnp.float32)
    y = xf * lax.rsqrt(jnp.mean(jnp.square(xf), axis=-1, keepdims=True) + LN_EPS)
    return (y * g.astype(jnp.float32)).astype(x.dtype)


def rope_tables(positions):
    inv_freq = ROPE_THETA ** (-jnp.arange(0, ROT_DIM, 2, dtype=jnp.float32) / ROT_DIM)
    ang = positions.astype(jnp.float32)[..., None] * inv_freq
    return jnp.cos(ang)[:, :, None, :], jnp.sin(ang)[:, :, None, :]


def apply_partial_rope(t, cos, sin):
    cos = cos.astype(t.dtype)
    sin = sin.astype(t.dtype)
    half = ROT_DIM // 2
    r1, r2, rest = t[..., :half], t[..., half:ROT_DIM], t[..., ROT_DIM:]
    return jnp.concatenate([r1 * cos - r2 * sin, r2 * cos + r1 * sin, rest], axis=-1)


def diff_attention(x, cos, sin, w_qkv, lq1, lk1, lq2, lk2, subln_g, w_o, lambda_init):
    B, S, _ = x.shape
    nq = S // Q_BLOCK
    qk_w = N_HEADS * 2 * HEAD_DIM
    qkv = x @ w_qkv
    q, k, v = jnp.split(qkv, [qk_w, 2 * qk_w], axis=-1)
    q = apply_partial_rope(q.reshape(B, S, 2 * N_HEADS, HEAD_DIM), cos, sin) * (HEAD_DIM ** -0.5)
    k = apply_partial_rope(k.reshape(B, S, 2 * N_HEADS, HEAD_DIM), cos, sin)
    q_blocks = q.reshape(B, nq, Q_BLOCK, N_HEADS, 2, HEAD_DIM).transpose(1, 0, 3, 4, 2, 5)
    k = k.reshape(B, S, N_HEADS, 2, HEAD_DIM).transpose(0, 2, 3, 1, 4)
    v = v.reshape(B, S, N_HEADS, V_DIM).transpose(0, 2, 1, 3)
    lam = (jnp.exp(jnp.sum(lq1.astype(jnp.float32) * lk1.astype(jnp.float32)))
           - jnp.exp(jnp.sum(lq2.astype(jnp.float32) * lk2.astype(jnp.float32)))
           + lambda_init)
    key_idx = jnp.arange(S)
    starts = jnp.arange(nq) * Q_BLOCK

    def block(args):
        qb, start = args
        s = jnp.einsum('bhcqd,bhckd->bhcqk', qb, k).astype(jnp.float32)
        q_idx = start + jnp.arange(Q_BLOCK)
        causal = key_idx[None, :] <= q_idx[:, None]
        p = jax.nn.softmax(jnp.where(causal, s, -jnp.inf), axis=-1)
        a = p[:, :, 0] - lam * p[:, :, 1]
        o = jnp.einsum('bhqk,bhkv->bhqv', a.astype(v.dtype), v)
        return rms_norm(o, subln_g) * (1.0 - lambda_init)

    o = lax.map(block, (q_blocks, starts))
    o = o.transpose(1, 0, 3, 2, 4).reshape(B, S, N_HEADS * V_DIM)
    return o @ w_o


def conformer_conv(x, w_in, b_in, w_dw, b_dw, ln_g, ln_b, w_out, b_out):
    h = x @ w_in + b_in
    a, g = jnp.split(h, 2, axis=-1)
    h = a * jax.nn.sigmoid(g)
    h = lax.conv_general_dilated(
        h, w_dw[:, None, :], window_strides=(1,), padding=[(CONV_WIDTH - 1, 0)],
        dimension_numbers=('NWC', 'WIO', 'NWC'), feature_group_count=CONV_DIM) + b_dw
    h = jax.nn.silu(layer_norm(h, ln_g, ln_b))
    return h @ w_out + b_out


def swiglu(h, w_gu, w_down):
    g, u = jnp.split(h @ w_gu, 2, axis=-1)
    return (jax.nn.silu(g) * u) @ w_down


def moe_swiglu(x, w_router, w_gu, w_down):
    B, S, D = x.shape
    T = B * S
    A = T * TOP_K
    nb = -(-A // EXPERT_BLOCK) + N_EXPERTS
    P = nb * EXPERT_BLOCK
    xf = x.reshape(T, D)
    logits = (xf @ w_router).astype(jnp.float32)
    top_v, top_i = lax.top_k(logits, TOP_K)
    gates = jax.nn.softmax(top_v, axis=-1).astype(x.dtype)
    e_flat = top_i.reshape(A)
    g_flat = gates.reshape(A)
    tok_flat = jnp.arange(A, dtype=jnp.int32) // TOP_K
    counts = jnp.zeros((N_EXPERTS,), jnp.int32).at[e_flat].add(1)
    starts = jnp.cumsum(counts) - counts
    padded = (counts + EXPERT_BLOCK - 1) // EXPERT_BLOCK * EXPERT_BLOCK
    pends = jnp.cumsum(padded)
    pstarts = pends - padded
    order = jnp.argsort(e_flat)
    e_sorted = e_flat[order]
    dest = pstarts[e_sorted] + (jnp.arange(A, dtype=jnp.int32) - starts[e_sorted])
    slot_tok = jnp.zeros((P,), jnp.int32).at[dest].set(tok_flat[order])
    slot_gate = jnp.zeros((P,), x.dtype).at[dest].set(g_flat[order])
    block_start = jnp.arange(nb, dtype=jnp.int32) * EXPERT_BLOCK
    block_exp = jnp.clip(jnp.searchsorted(pends, block_start, side='right'), 0, N_EXPERTS - 1)

    def expert_block(args):
        tok, gate, ex = args
        return swiglu(xf[tok], w_gu[ex], w_down[ex]) * gate[:, None]

    ys = lax.map(expert_block, (slot_tok.reshape(nb, EXPERT_BLOCK),
                                slot_gate.reshape(nb, EXPERT_BLOCK), block_exp))
    y = jax.ops.segment_sum(ys.reshape(P, D), slot_tok, num_segments=T)
    return y.reshape(B, S, D)


def setup_inputs(seed: int = 0) -> dict:
    key = jax.random.key(seed)
    ks = iter(jax.random.split(key, 40))
    f32 = jnp.float32

    def nrm(shape, scale):
        return jax.random.normal(next(ks), shape, f32) * scale

    def gain(n):
        return 1.0 + nrm((n,), 0.02)

    D = D_MODEL
    qkv_w = 2 * (N_HEADS * 2 * HEAD_DIM) + N_HEADS * V_DIM
    inp = {}
    inp['x'] = nrm((BATCH, SEQ, D), 1.0)
    inp['positions'] = (jnp.arange(SEQ, dtype=jnp.int32)[None, :]
                        + jax.random.randint(next(ks), (BATCH, 1), 0, 4096, jnp.int32))
    inp['l0_w_qkv'] = nrm((D, qkv_w), D ** -0.5)
    inp['l0_lambda_q1'] = nrm((HEAD_DIM,), 0.1)
    inp['l0_lambda_k1'] = nrm((HEAD_DIM,), 0.1)
    inp['l0_lambda_q2'] = nrm((HEAD_DIM,), 0.1)
    inp['l0_lambda_k2'] = nrm((HEAD_DIM,), 0.1)
    inp['l0_subln_g'] = gain(V_DIM)
    inp['l0_w_o'] = nrm((N_HEADS * V_DIM, D), (N_HEADS * V_DIM) ** -0.5 * BETA)
    inp['l0_ln1_g'] = gain(D)
    inp['l0_ln1_b'] = nrm((D,), 0.02)
    inp['l0_ffn_w_gu'] = nrm((D, 2 * D_FF), D ** -0.5)
    inp['l0_ffn_w_down'] = nrm((D_FF, D), D_FF ** -0.5 * BETA)
    inp['l0_ln2_g'] = gain(D)
    inp['l0_ln2_b'] = nrm((D,), 0.02)
    inp['l1_conv_w_in'] = nrm((D, 2 * CONV_DIM), D ** -0.5)
    inp['l1_conv_b_in'] = nrm((2 * CONV_DIM,), 0.02)
    inp['l1_conv_w_dw'] = nrm((CONV_WIDTH, CONV_DIM), CONV_WIDTH ** -0.5)
    inp['l1_conv_b_dw'] = nrm((CONV_DIM,), 0.02)
    inp['l1_conv_ln_g'] = gain(CONV_DIM)
    inp['l1_conv_ln_b'] = nrm((CONV_DIM,), 0.02)
    inp['l1_conv_w_out'] = nrm((CONV_DIM, D), CONV_DIM ** -0.5 * BETA)
    inp['l1_conv_b_out'] = nrm((D,), 0.02)
    inp['l1_ln1_g'] = gain(D)
    inp['l1_ln1_b'] = nrm((D,), 0.02)
    inp['l1_moe_w_router'] = nrm((D, N_EXPERTS), D ** -0.5)
    inp['l1_moe_w_gu'] = nrm((N_EXPERTS, D, 2 * D_FF_EXPERT), D ** -0.5)
    inp['l1_moe_w_down'] = nrm((N_EXPERTS, D_FF_EXPERT, D), D_FF_EXPERT ** -0.5 * BETA)
    inp['l1_ln2_g'] = gain(D)
    inp['l1_ln2_b'] = nrm((D,), 0.02)
    return inp


def reference(x, positions,
              l0_w_qkv, l0_lambda_q1, l0_lambda_k1, l0_lambda_q2, l0_lambda_k2, l0_subln_g, l0_w_o,
              l0_ln1_g, l0_ln1_b, l0_ffn_w_gu, l0_ffn_w_down, l0_ln2_g, l0_ln2_b,
              l1_conv_w_in, l1_conv_b_in, l1_conv_w_dw, l1_conv_b_dw, l1_conv_ln_g, l1_conv_ln_b,
              l1_conv_w_out, l1_conv_b_out, l1_ln1_g, l1_ln1_b,
              l1_moe_w_router, l1_moe_w_gu, l1_moe_w_down, l1_ln2_g, l1_ln2_b):
    cos, sin = rope_tables(positions)
    layers = [
        dict(mix=lambda h: diff_attention(h, cos, sin, l0_w_qkv, l0_lambda_q1, l0_lambda_k1,
                                          l0_lambda_q2, l0_lambda_k2, l0_subln_g, l0_w_o,
                                          0.8 - 0.6 * math.exp(-0.3 * 0)),
             ln1=(l0_ln1_g, l0_ln1_b),
             ffn=lambda h: swiglu(h, l0_ffn_w_gu, l0_ffn_w_down),
             ln2=(l0_ln2_g, l0_ln2_b)),
        dict(mix=lambda h: conformer_conv(h, l1_conv_w_in, l1_conv_b_in, l1_conv_w_dw, l1_conv_b_dw,
                                          l1_conv_ln_g, l1_conv_ln_b, l1_conv_w_out, l1_conv_b_out),
             ln1=(l1_ln1_g, l1_ln1_b),
             ffn=lambda h: moe_swiglu(h, l1_moe_w_router, l1_moe_w_gu, l1_moe_w_down),
             ln2=(l1_ln2_g, l1_ln2_b)),
    ]
    for i in range(DEPTH):
        layer = layers[i]
        x = layer_norm(ALPHA * x + layer['mix'](x), *layer['ln1'])
        x = layer_norm(ALPHA * x + layer['ffn'](x), *layer['ln2'])
    return x
```

```python
import functools
import math

import jax
import jax.numpy as jnp
from jax import lax
from jax.experimental import pallas as pl
from jax.experimental.pallas import tpu as pltpu

D_MODEL = 2048
DEPTH = 2
N_HEADS = 16
HEAD_DIM = 64
V_DIM = 128
ROT_DIM = 16
ROPE_THETA = 500000.0
CONV_WIDTH = 31
D_FF = 5632
N_EXPERTS = 8
TOP_K = 2
D_FF_EXPERT = 7168
ALPHA = (2.0 * DEPTH) ** 0.25
LN_EPS = 1e-5
LAMBDA_INIT = 0.8 - 0.6 * math.exp(-0.3 * 0)

LANES = 128
VMEM_LIMIT = 56 * 1024 * 1024

F32 = jnp.float32
BF16 = jnp.bfloat16


def _params(sem):
    return pltpu.CompilerParams(dimension_semantics=sem, vmem_limit_bytes=VMEM_LIMIT)


def _resident(shape):
    nd = len(shape)
    return pl.BlockSpec(shape, lambda *_: (0,) * nd, pipeline_mode=pl.Buffered(1))


def _layer_norm(y, g, b):
    mu = jnp.mean(y, axis=-1, keepdims=True)
    yc = y - mu
    var = jnp.mean(yc * yc, axis=-1, keepdims=True)
    return yc * lax.rsqrt(var + LN_EPS) * g + b


QKV_TM = 512
QKV_TN = 512


def _qkv_kernel(x_ref, wq_ref, wk_ref, wv_ref, c_ref, sa_ref, sb_ref,
                q_ref, k1_ref, k2_ref, v_ref):
    xb = x_ref[...].astype(BF16)
    c = c_ref[...]
    sa = sa_ref[...]
    sb = sb_ref[...]
    lane = lax.broadcasted_iota(jnp.int32, (1, LANES), 1)
    first_map = lane < HEAD_DIM
    scale = HEAD_DIM ** -0.5

    def rope(t):
        return t * c + pltpu.roll(t, 8, 1) * sa + pltpu.roll(t, LANES - 8, 1) * sb

    for n in range(D_MODEL // QKV_TN):
        cols = slice(n * QKV_TN, (n + 1) * QKV_TN)
        q = jnp.dot(xb, wq_ref[:, cols], preferred_element_type=F32)
        k = jnp.dot(xb, wk_ref[:, cols], preferred_element_type=F32)
        v = jnp.dot(xb, wv_ref[:, cols], preferred_element_type=F32)
        v_ref[:, cols] = v.astype(BF16)
        for h in range(QKV_TN // LANES):
            hs = slice(h * LANES, (h + 1) * LANES)
            os = slice(n * QKV_TN + h * LANES, n * QKV_TN + (h + 1) * LANES)
            q_ref[:, os] = (rope(q[:, hs]) * scale).astype(BF16)
            kr = rope(k[:, hs])
            k1_ref[:, os] = jnp.where(first_map, kr, 0.0).astype(BF16)
            k2_ref[:, os] = jnp.where(first_map, 0.0, kr).astype(BF16)


def _qkv_proj(x2, wq, wk, wv, cos_t, sin_a, sin_b):
    T = x2.shape[0]
    row = lambda i: (i, 0)
    out = jax.ShapeDtypeStruct((T, D_MODEL), BF16)
    return pl.pallas_call(
        _qkv_kernel,
        out_shape=(out, out, out, out),
        grid=(T // QKV_TM,),
        in_specs=[
            pl.BlockSpec((QKV_TM, D_MODEL), row),
            _resident((D_MODEL, D_MODEL)),
            _resident((D_MODEL, D_MODEL)),
            _resident((D_MODEL, D_MODEL)),
            pl.BlockSpec((QKV_TM, LANES), row),
            pl.BlockSpec((QKV_TM, LANES), row),
            pl.BlockSpec((QKV_TM, LANES), row),
        ],
        out_specs=(pl.BlockSpec((QKV_TM, D_MODEL), row),) * 4,
        compiler_params=_params(("arbitrary",)),
        name="qkv_rope",
    )(x2, wq, wk, wv, cos_t, sin_a, sin_b)


ATT_TQ = 512
ATT_TK = 512


def _attn_kernel(q_ref, k1_ref, k2_ref, v_ref, lq1_ref, lk1_ref, lq2_ref, lk2_ref,
                 g_ref, o_ref):
    i = pl.program_id(2)
    q = q_ref[...]
    nt = (((1,), (1,)), ((), ()))

    def one_map(k_ref, ks, v, m, l, acc, mask):
        k = k_ref[pl.ds(ks, ATT_TK), :]
        s = lax.dot_general(q, k, nt, preferred_element_type=F32)
        if mask is not None:
            s = jnp.where(mask, s, -jnp.inf)
        m_new = jnp.maximum(m, jnp.max(s, axis=1, keepdims=True))
        alpha = jnp.exp(m - m_new)
        p = jnp.exp(s - m_new)
        l_new = alpha * l + jnp.sum(p, axis=1, keepdims=True)
        acc_new = alpha * acc + jnp.dot(p.astype(BF16), v, preferred_element_type=F32)
        return m_new, l_new, acc_new

    def step(j, carry, mask=None):
        m1, l1, a1, m2, l2, a2 = carry
        ks = pl.multiple_of(j * ATT_TK, ATT_TK)
        v = v_ref[pl.ds(ks, ATT_TK), :]
        m1, l1, a1 = one_map(k1_ref, ks, v, m1, l1, a1, mask)
        m2, l2, a2 = one_map(k2_ref, ks, v, m2, l2, a2, mask)
        return m1, l1, a1, m2, l2, a2

    neg = jnp.full((ATT_TQ, 1), -jnp.inf, F32)
    zero = jnp.zeros((ATT_TQ, 1), F32)
    zacc = jnp.zeros((ATT_TQ, V_DIM), F32)
    carry = lax.fori_loop(0, i, step, (neg, zero, zacc, neg, zero, zacc))
    rows = lax.broadcasted_iota(jnp.int32, (ATT_TQ, ATT_TK), 0)
    cols = lax.broadcasted_iota(jnp.int32, (ATT_TQ, ATT_TK), 1)
    m1, l1, a1, m2, l2, a2 = step(i, carry, mask=cols <= rows)

    lam = (jnp.exp(jnp.sum(lq1_ref[...] * lk1_ref[...], axis=1, keepdims=True))
           - jnp.exp(jnp.sum(lq2_ref[...] * lk2_ref[...], axis=1, keepdims=True))
           + LAMBDA_INIT)
    o = a1 / l1 - lam * (a2 / l2)
    o = o * lax.rsqrt(jnp.mean(o * o, axis=-1, keepdims=True) + LN_EPS) * g_ref[...]
    o_ref[...] = (o * (1.0 - LAMBDA_INIT)).astype(BF16)


def _attention(q, k1, k2, v, lq1, lk1, lq2, lk2, g, batch, seq):
    T = q.shape[0]
    nq = seq // ATT_TQ
    qmap = lambda b, h, i: (b * nq + i, h)
    kvmap = lambda b, h, i: (b, h)
    vec = lambda n: pl.BlockSpec((1, n), lambda b, h, i: (0, 0))
    return pl.pallas_call(
        _attn_kernel,
        out_shape=jax.ShapeDtypeStruct((T, D_MODEL), BF16),
        grid=(batch, N_HEADS, nq),
        in_specs=[
            pl.BlockSpec((ATT_TQ, LANES), qmap),
            pl.BlockSpec((seq, LANES), kvmap),
            pl.BlockSpec((seq, LANES), kvmap),
            pl.BlockSpec((seq, LANES), kvmap),
            vec(HEAD_DIM), vec(HEAD_DIM), vec(HEAD_DIM), vec(HEAD_DIM),
            vec(V_DIM),
        ],
        out_specs=pl.BlockSpec((ATT_TQ, LANES), qmap),
        compiler_params=_params(("arbitrary", "arbitrary", "arbitrary")),
        name="diff_attention",
    )(q, k1, k2, v, lq1, lk1, lq2, lk2, g)


PROJ_TM = 512
PROJ_TN = 512


def _proj_ln_kernel(a_ref, w_ref, x_ref, g_ref, b_ref, o_ref):
    a = a_ref[...]
    for n in range(D_MODEL // PROJ_TN):
        cols = slice(n * PROJ_TN, (n + 1) * PROJ_TN)
        y = jnp.dot(a, w_ref[:, cols], preferred_element_type=F32)
        o_ref[:, cols] = ALPHA * x_ref[:, cols] + y
    o_ref[...] = _layer_norm(o_ref[...], g_ref[...], b_ref[...])


def _proj_ln(a, w, x2, g, b):
    T = a.shape[0]
    row = lambda i: (i, 0)
    vec = pl.BlockSpec((1, D_MODEL), lambda i: (0, 0))
    return pl.pallas_call(
        _proj_ln_kernel,
        out_shape=jax.ShapeDtypeStruct((T, D_MODEL), F32),
        grid=(T // PROJ_TM,),
        in_specs=[
            pl.BlockSpec((PROJ_TM, D_MODEL), row),
            _resident((D_MODEL, D_MODEL)),
            pl.BlockSpec((PROJ_TM, D_MODEL), row),
            vec, vec,
        ],
        out_specs=pl.BlockSpec((PROJ_TM, D_MODEL), row),
        compiler_params=_params(("arbitrary",)),
        name="proj_residual_ln",
    )(a, w, x2, g, b)


FFN_TM = 512
FFN_TF = 512


def _ffn_kernel(x_ref, wg_ref, wu_ref, wd_ref, g_ref, b_ref, o_ref, xb_ref, acc_ref):
    f = pl.program_id(1)

    @pl.when(f == 0)
    def _():
        xb_ref[...] = x_ref[...].astype(BF16)
        acc_ref[...] = jnp.zeros_like(acc_ref)

    xb = xb_ref[...]
    gate = jnp.dot(xb, wg_ref[...], preferred_element_type=F32)
    up = jnp.dot(xb, wu_ref[...], preferred_element_type=F32)
    act = (gate * jax.nn.sigmoid(gate) * up).astype(BF16)
    acc_ref[...] += jnp.dot(act, wd_ref[...], preferred_element_type=F32)

    @pl.when(f == pl.num_programs(1) - 1)
    def _():
        y = ALPHA * x_ref[...] + acc_ref[...]
        o_ref[...] = _layer_norm(y, g_ref[...], b_ref[...])


def _ffn(x2, w_gu, w_down, g, b):
    T = x2.shape[0]
    nf = D_FF // FFN_TF
    row = lambda i, f: (i, 0)
    vec = pl.BlockSpec((1, D_MODEL), lambda i, f: (0, 0))
    return pl.pallas_call(
        _ffn_kernel,
        out_shape=jax.ShapeDtypeStruct((T, D_MODEL), F32),
        grid=(T // FFN_TM, nf),
        in_specs=[
            pl.BlockSpec((FFN_TM, D_MODEL), row),
            pl.BlockSpec((D_MODEL, FFN_TF), lambda i, f: (0, f)),
            pl.BlockSpec((D_MODEL, FFN_TF), lambda i, f: (0, nf + f)),
            pl.BlockSpec((FFN_TF, D_MODEL), lambda i, f: (f, 0)),
            vec, vec,
        ],
        out_specs=pl.BlockSpec((FFN_TM, D_MODEL), row),
        scratch_shapes=[pltpu.VMEM((FFN_TM, D_MODEL), BF16),
                        pltpu.VMEM((FFN_TM, D_MODEL), F32)],
        compiler_params=_params(("arbitrary", "arbitrary")),
        name="dense_swiglu_ln",
    )(x2, w_gu, w_gu, w_down, g, b)


GLU_TM = 512
GLU_TN = 512


def _glu_kernel(x_ref, w_ref, b_ref, o_ref):
    xb = x_ref[...].astype(BF16)
    for n in range(D_MODEL // GLU_TN):
        cols = slice(n * GLU_TN, (n + 1) * GLU_TN)
        gcols = slice(D_MODEL + n * GLU_TN, D_MODEL + (n + 1) * GLU_TN)
        a = jnp.dot(xb, w_ref[:, cols], preferred_element_type=F32) + b_ref[:, cols]
        g = jnp.dot(xb, w_ref[:, gcols], preferred_element_type=F32) + b_ref[:, gcols]
        o_ref[:, cols] = a * jax.nn.sigmoid(g)


def _conv_glu(x2, w_in, b_in):
    T = x2.shape[0]
    row = lambda i: (i, 0)
    return pl.pallas_call(
        _glu_kernel,
        out_shape=jax.ShapeDtypeStruct((T, D_MODEL), F32),
        grid=(T // GLU_TM,),
        in_specs=[
            pl.BlockSpec((GLU_TM, D_MODEL), row),
            _resident((D_MODEL, 2 * D_MODEL)),
            pl.BlockSpec((1, 2 * D_MODEL), lambda i: (0, 0)),
        ],
        out_specs=pl.BlockSpec((GLU_TM, D_MODEL), row),
        compiler_params=_params(("arbitrary",)),
        name="conv_in_glu",
    )(x2, w_in, b_in)


CONV_TM = 512
CONV_HALO = 32
CONV_RB = 32
CONV_CB = 512
CONV_PB = 64
SUBLANES = 8
CONV_PH_ROWS = CONV_TM + CONV_HALO - SUBLANES


def _conv_kernel(cur_ref, prev_ref, wdw_ref, bdw_ref, cg_ref, cb_ref, wo_ref, bo_ref,
                 x_ref, g_ref, b_ref, o_ref, ext_ref, ph_ref, *, seq):
    i = pl.program_id(0)
    seq_start = (i * CONV_TM) % seq == 0
    ext_ref[0:CONV_HALO, :] = jnp.where(seq_start, 0.0, prev_ref[...])
    ext_ref[CONV_HALO:, :] = cur_ref[...]
    lead = CONV_HALO - (CONV_WIDTH - 1)

    for cb in range(D_MODEL // CONV_CB):
        cols = slice(cb * CONV_CB, (cb + 1) * CONV_CB)

        def shifted_copies(r, _):
            r0 = pl.multiple_of(r * CONV_PB, CONV_PB)
            win = ext_ref[pl.ds(r0, CONV_PB + SUBLANES), cols]
            for s in range(1, SUBLANES):
                ph_ref[s - 1, pl.ds(r0, CONV_PB), :] = win[s:s + CONV_PB, :]
            return 0

        lax.fori_loop(0, CONV_TM // CONV_PB, shifted_copies, 0)
        tail = CONV_PH_ROWS - CONV_TM
        win = ext_ref[CONV_TM:CONV_TM + tail + SUBLANES, cols]
        for s in range(1, SUBLANES):
            ph_ref[s - 1, CONV_TM:CONV_PH_ROWS, :] = win[s:s + tail, :]

        def rows_block(r, _):
            r0 = pl.multiple_of(r * CONV_RB, CONV_RB)
            acc = jnp.zeros((CONV_RB, CONV_CB), F32)
            for k in range(CONV_WIDTH):
                s = (lead + k) % SUBLANES
                base = r0 + (lead + k - s)
                if s == 0:
                    src = ext_ref[pl.ds(base, CONV_RB), cols]
                else:
                    src = ph_ref[s - 1, pl.ds(base, CONV_RB), :]
                acc = acc + wdw_ref[k:k + 1, cols] * src
            o_ref[pl.ds(r0, CONV_RB), cols] = acc + bdw_ref[:, cols]
            return 0

        lax.fori_loop(0, CONV_TM // CONV_RB, rows_block, 0)

    h = _layer_norm(o_ref[...], cg_ref[...], cb_ref[...])
    hb = (h * jax.nn.sigmoid(h)).astype(BF16)
    for n in range(D_MODEL // PROJ_TN):
        cols = slice(n * PROJ_TN, (n + 1) * PROJ_TN)
        y = jnp.dot(hb, wo_ref[:, cols], preferred_element_type=F32)
        o_ref[:, cols] = ALPHA * x_ref[:, cols] + (y + bo_ref[:, cols])
    o_ref[...] = _layer_norm(o_ref[...], g_ref[...], b_ref[...])


def _conv_module(glu, w_dw, b_dw, cg, cb, w_out, b_out, x2, g, b, seq):
    T = glu.shape[0]
    row = lambda i: (i, 0)
    ratio = CONV_TM // CONV_HALO
    vec = pl.BlockSpec((1, D_MODEL), lambda i: (0, 0))
    return pl.pallas_call(
        functools.partial(_conv_kernel, seq=seq),
        out_shape=jax.ShapeDtypeStruct((T, D_MODEL), F32),
        grid=(T // CONV_TM,),
        in_specs=[
            pl.BlockSpec((CONV_TM, D_MODEL), row),
            pl.BlockSpec((CONV_HALO, D_MODEL), lambda i: (jnp.maximum(i * ratio - 1, 0), 0)),
            pl.BlockSpec((CONV_HALO, D_MODEL), lambda i: (0, 0)),
            vec, vec, vec,
            _resident((D_MODEL, D_MODEL)),
            vec,
            pl.BlockSpec((CONV_TM, D_MODEL), row),
            vec, vec,
        ],
        out_specs=pl.BlockSpec((CONV_TM, D_MODEL), row),
        scratch_shapes=[pltpu.VMEM((CONV_HALO + CONV_TM, D_MODEL), F32),
                        pltpu.VMEM((SUBLANES - 1, CONV_PH_ROWS, CONV_CB), F32)],
        compiler_params=_params(("arbitrary",)),
        name="dwconv_out_ln",
    )(glu, glu, w_dw, b_dw, cg, cb, w_out, b_out, x2, g, b)


RT_TM = 512
R_E1, R_E2, R_RANK1, R_RANK2, R_G1, R_G2 = 0, 1, 2, 3, 4, 5


def _router_kernel(x_ref, wr_ref, route_ref, count_ref, carry_ref):
    i = pl.program_id(0)

    @pl.when(i == 0)
    def _():
        carry_ref[...] = jnp.zeros_like(carry_ref)

    logits = jnp.dot(x_ref[...], wr_ref[...], preferred_element_type=F32,
                     precision=lax.Precision.HIGHEST)
    lane = lax.broadcasted_iota(jnp.int32, (RT_TM, LANES), 1).astype(F32)
    logits = jnp.where(lane < N_EXPERTS, logits, -jnp.inf)
    v1 = jnp.max(logits, axis=1, keepdims=True)
    e1 = jnp.min(jnp.where(logits == v1, lane, float(LANES)), axis=1, keepdims=True)
    rest = jnp.where(lane == e1, -jnp.inf, logits)
    v2 = jnp.max(rest, axis=1, keepdims=True)
    e2 = jnp.min(jnp.where(rest == v2, lane, float(LANES)), axis=1, keepdims=True)
    ex = jnp.exp(v2 - v1)
    g1 = 1.0 / (1.0 + ex)
    g2 = ex / (1.0 + ex)

    member = jnp.logical_or(lane == e1, lane == e2)
    r = lax.broadcasted_iota(jnp.int32, (RT_TM, RT_TM), 0)
    c = lax.broadcasted_iota(jnp.int32, (RT_TM, RT_TM), 1)
    before = (c < r).astype(BF16)
    prior = jnp.dot(before, member.astype(BF16), preferred_element_type=F32) + carry_ref[0:1, :]
    rank1 = jnp.sum(jnp.where(lane == e1, prior, 0.0), axis=1, keepdims=True)
    rank2 = jnp.sum(jnp.where(lane == e2, prior, 0.0), axis=1, keepdims=True)
    total = carry_ref[0:1, :] + jnp.sum(member.astype(F32), axis=0, keepdims=True)
    carry_ref[...] = jnp.broadcast_to(total, carry_ref.shape)
    count_ref[...] = jnp.broadcast_to(total, count_ref.shape)

    rec = jnp.where(lane == R_E1, e1, 0.0)
    rec = jnp.where(lane == R_E2, e2, rec)
    rec = jnp.where(lane == R_RANK1, rank1, rec)
    rec = jnp.where(lane == R_RANK2, rank2, rec)
    rec = jnp.where(lane == R_G1, g1, rec)
    rec = jnp.where(lane == R_G2, g2, rec)
    route_ref[...] = rec


def _router(x2, w_router_padded):
    T = x2.shape[0]
    return pl.pallas_call(
        _router_kernel,
        out_shape=(jax.ShapeDtypeStruct((T, LANES), F32),
                   jax.ShapeDtypeStruct((8, LANES), F32)),
        grid=(T // RT_TM,),
        in_specs=[
            pl.BlockSpec((RT_TM, D_MODEL), lambda i: (i, 0)),
            pl.BlockSpec((D_MODEL, LANES), lambda i: (0, 0)),
        ],
        out_specs=(pl.BlockSpec((RT_TM, LANES), lambda i: (i, 0)),
                   pl.BlockSpec((8, LANES), lambda i: (0, 0))),
        scratch_shapes=[pltpu.VMEM((8, LANES), F32)],
        compiler_params=_params(("arbitrary",)),
        name="router_top2",
    )(x2, w_router_padded)


DISP_TM = 256


def _dispatch_kernel(dest_ref, x_ref, xs_ref, sem):
    def row_copy(r, slot):
        return pltpu.make_async_copy(x_ref.at[pl.ds(r, 1), :],
                                     xs_ref.at[pl.ds(slot, 1), :], sem)

    def issue(r, _):
        row_copy(r, dest_ref[0, 0, 2 * r]).start()
        row_copy(r, dest_ref[0, 0, 2 * r + 1]).start()
        return 0

    lax.fori_loop(0, DISP_TM, issue, 0)

    def drain(r, _):
        row_copy(r, dest_ref[0, 0, 2 * r]).wait()
        row_copy(r, dest_ref[0, 0, 2 * r + 1]).wait()
        return 0

    lax.fori_loop(0, DISP_TM, drain, 0)


def _dispatch(x2, dest):
    T = x2.shape[0]
    n = T // DISP_TM
    dest3 = dest.reshape(n, 1, TOP_K * DISP_TM)
    return pl.pallas_call(
        _dispatch_kernel,
        out_shape=jax.ShapeDtypeStruct((T * TOP_K, D_MODEL), F32),
        grid=(n,),
        in_specs=[
            pl.BlockSpec((1, 1, TOP_K * DISP_TM), lambda i: (i, 0, 0),
                         memory_space=pltpu.SMEM),
            pl.BlockSpec((DISP_TM, D_MODEL), lambda i: (i, 0)),
        ],
        out_specs=pl.BlockSpec(memory_space=pl.ANY),
        scratch_shapes=[pltpu.SemaphoreType.DMA],
        compiler_params=_params(("arbitrary",)),
        name="moe_dispatch",
    )(dest3, x2)


EXP_TM = 512
EXP_TF = 1024


def _expert_kernel(tile_ref, exp_ref, lo_ref, hi_ref, first_ref, valid_ref,
                   xs_ref, wg_ref, wu_ref, wd_ref, ys_ref, xb_ref):
    w = pl.program_id(0)
    f = pl.program_id(1)
    valid = valid_ref[w] == 1

    @pl.when(jnp.logical_and(f == 0, first_ref[w] == 1))
    def _():
        ys_ref[...] = jnp.zeros_like(ys_ref)

    @pl.when(jnp.logical_and(f == 0, valid))
    def _():
        xb_ref[...] = xs_ref[...].astype(BF16)

    @pl.when(valid)
    def _():
        xb = xb_ref[...]
        gate = jnp.dot(xb, wg_ref[...], preferred_element_type=F32)
        up = jnp.dot(xb, wu_ref[...], preferred_element_type=F32)
        rows = lax.broadcasted_iota(jnp.int32, (EXP_TM, 1), 0)
        mine = jnp.logical_and(rows >= lo_ref[w], rows < hi_ref[w])
        act = jnp.where(mine, gate * jax.nn.sigmoid(gate) * up, 0.0).astype(BF16)
        ys_ref[...] += jnp.dot(act, wd_ref[...], preferred_element_type=F32)


def _experts(xs, w_gu, w_down, work):
    A = xs.shape[0]
    nf = D_FF_EXPERT // EXP_TF
    n_work = work[0].shape[0]

    def fidx(w, f, valid_ref):
        return jnp.where(valid_ref[w] == 1, f, nf - 1)

    def xmap(w, f, tile, exp, lo, hi, first, valid):
        return (tile[w], 0)

    def gmap(w, f, tile, exp, lo, hi, first, valid):
        return (exp[w], 0, fidx(w, f, valid))

    def umap(w, f, tile, exp, lo, hi, first, valid):
        return (exp[w], 0, nf + fidx(w, f, valid))

    def dmap(w, f, tile, exp, lo, hi, first, valid):
        return (exp[w], fidx(w, f, valid), 0)

    return pl.pallas_call(
        _expert_kernel,
        out_shape=jax.ShapeDtypeStruct((A, D_MODEL), F32),
        grid_spec=pltpu.PrefetchScalarGridSpec(
            num_scalar_prefetch=6,
            grid=(n_work, nf),
            in_specs=[
                pl.BlockSpec((EXP_TM, D_MODEL), xmap),
                pl.BlockSpec((None, D_MODEL, EXP_TF), gmap),
                pl.BlockSpec((None, D_MODEL, EXP_TF), umap),
                pl.BlockSpec((None, EXP_TF, D_MODEL), dmap),
            ],
            out_specs=pl.BlockSpec((EXP_TM, D_MODEL), xmap),
            scratch_shapes=[pltpu.VMEM((EXP_TM, D_MODEL), BF16)],
        ),
        compiler_params=_params(("arbitrary", "arbitrary")),
        name="moe_experts",
    )(*work, xs, w_gu, w_gu, w_down)


def _work_list(counts, n_tiles):
    n_work = n_tiles + N_EXPERTS - 1
    ends = jnp.cumsum(counts)
    starts = ends - counts
    first_tile = starts // EXP_TM
    last_tile = jnp.maximum(ends - 1, 0) // EXP_TM
    n_item = jnp.where(counts > 0, last_tile - first_tile + 1, 0)
    item_end = jnp.cumsum(n_item)
    item_start = item_end - n_item
    total = item_end[-1]
    w = jnp.arange(n_work, dtype=jnp.int32)
    wc = jnp.minimum(w, total - 1)
    exp = jnp.searchsorted(item_end, wc, side='right').astype(jnp.int32)
    tile = (first_tile[exp] + (wc - item_start[exp])).astype(jnp.int32)
    lo = jnp.clip(starts[exp] - tile * EXP_TM, 0, EXP_TM).astype(jnp.int32)
    hi = jnp.clip(ends[exp] - tile * EXP_TM, 0, EXP_TM).astype(jnp.int32)
    valid = (w < total).astype(jnp.int32)
    prev_tile = jnp.concatenate([jnp.full((1,), -1, jnp.int32), tile[:-1]])
    first = jnp.logical_and(tile != prev_tile, valid == 1).astype(jnp.int32)
    return tile, exp, lo, hi, first, valid


COMB_TM = 256


def _combine_kernel(dest_ref, ys_ref, route_ref, x_ref, g_ref, b_ref, o_ref,
                    rows_ref, sem):
    def row_copy(r, k):
        slot = dest_ref[0, 0, 2 * r + k]
        return pltpu.make_async_copy(ys_ref.at[pl.ds(slot, 1), :],
                                     rows_ref.at[k, pl.ds(r, 1), :], sem)

    def issue(r, _):
        row_copy(r, 0).start()
        row_copy(r, 1).start()
        return 0

    lax.fori_loop(0, COMB_TM, issue, 0)

    def drain(r, _):
        row_copy(r, 0).wait()
        row_copy(r, 1).wait()
        return 0

    lax.fori_loop(0, COMB_TM, drain, 0)

    route = route_ref[...]
    lane = lax.broadcasted_iota(jnp.int32, route.shape, 1)
    g1 = jnp.sum(jnp.where(lane == R_G1, route, 0.0), axis=1, keepdims=True)
    g2 = jnp.sum(jnp.where(lane == R_G2, route, 0.0), axis=1, keepdims=True)
    y = rows_ref[0] * g1 + rows_ref[1] * g2
    o_ref[...] = _layer_norm(ALPHA * x_ref[...] + y, g_ref[...], b_ref[...])


def _combine(dest, ys, route, x2, g, b):
    T = x2.shape[0]
    n = T // COMB_TM
    dest3 = dest.reshape(n, 1, TOP_K * COMB_TM)
    vec = pl.BlockSpec((1, D_MODEL), lambda i: (0, 0))
    return pl.pallas_call(
        _combine_kernel,
        out_shape=jax.ShapeDtypeStruct((T, D_MODEL), F32),
        grid=(n,),
        in_specs=[
            pl.BlockSpec((1, 1, TOP_K * COMB_TM), lambda i: (i, 0, 0),
                         memory_space=pltpu.SMEM),
            pl.BlockSpec(memory_space=pl.ANY),
            pl.BlockSpec((COMB_TM, LANES), lambda i: (i, 0)),
            pl.BlockSpec((COMB_TM, D_MODEL), lambda i: (i, 0)),
            vec, vec,
        ],
        out_specs=pl.BlockSpec((COMB_TM, D_MODEL), lambda i: (i, 0)),
        scratch_shapes=[pltpu.VMEM((TOP_K, COMB_TM, D_MODEL), F32),
                        pltpu.SemaphoreType.DMA],
        compiler_params=_params(("arbitrary",)),
        name="moe_combine_ln",
    )(dest3, ys, route, x2, g, b)


def _rope_tables(positions):
    half = ROT_DIM // 2
    inv_freq = ROPE_THETA ** (-jnp.arange(0, ROT_DIM, 2, dtype=F32) / ROT_DIM)
    ang = positions.reshape(-1).astype(F32)[:, None] * inv_freq
    cos, sin = jnp.cos(ang), jnp.sin(ang)
    T = ang.shape[0]
    pad = HEAD_DIM - ROT_DIM
    one_map = lambda a, b, fill: jnp.concatenate(
        [a, b, jnp.full((T, pad), fill, F32)], axis=1)
    zeros = jnp.zeros((T, half), F32)
    cos_t = one_map(cos, cos, 1.0)
    sin_a = one_map(zeros, sin, 0.0)
    sin_b = one_map(-sin, zeros, 0.0)
    twice = lambda t: jnp.concatenate([t, t], axis=1)
    return twice(cos_t), twice(sin_a), twice(sin_b)


def kernel(x, positions, l0_w_qkv, l0_lambda_q1, l0_lambda_k1, l0_lambda_q2, l0_lambda_k2, l0_subln_g, l0_w_o, l0_ln1_g, l0_ln1_b, l0_ffn_w_gu, l0_ffn_w_down, l0_ln2_g, l0_ln2_b, l1_conv_w_in, l1_conv_b_in, l1_conv_w_dw, l1_conv_b_dw, l1_conv_ln_g, l1_conv_ln_b, l1_conv_w_out, l1_conv_b_out, l1_ln1_g, l1_ln1_b, l1_moe_w_router, l1_moe_w_gu, l1_moe_w_down, l1_ln2_g, l1_ln2_b):
    B, S, D = x.shape
    T = B * S
    x2 = x.reshape(T, D)
    vec = lambda a: a.reshape(1, -1).astype(F32)
    qk_w = N_HEADS * 2 * HEAD_DIM

    cos_t, sin_a, sin_b = _rope_tables(positions)
    wq = l0_w_qkv[:, :qk_w].astype(BF16)
    wk = l0_w_qkv[:, qk_w:2 * qk_w].astype(BF16)
    wv = l0_w_qkv[:, 2 * qk_w:].astype(BF16)
    q, k1, k2, v = _qkv_proj(x2, wq, wk, wv, cos_t, sin_a, sin_b)
    attn = _attention(q, k1, k2, v, vec(l0_lambda_q1), vec(l0_lambda_k1),
                      vec(l0_lambda_q2), vec(l0_lambda_k2), vec(l0_subln_g), B, S)
    h = _proj_ln(attn, l0_w_o.astype(BF16), x2, vec(l0_ln1_g), vec(l0_ln1_b))
    h = _ffn(h, l0_ffn_w_gu.astype(BF16), l0_ffn_w_down.astype(BF16),
             vec(l0_ln2_g), vec(l0_ln2_b))

    glu = _conv_glu(h, l1_conv_w_in.astype(BF16), vec(l1_conv_b_in))
    w_dw = jnp.concatenate(
        [l1_conv_w_dw.astype(F32), jnp.zeros((CONV_HALO - CONV_WIDTH, D), F32)], axis=0)
    h = _conv_module(glu, w_dw, vec(l1_conv_b_dw), vec(l1_conv_ln_g), vec(l1_conv_ln_b),
                     l1_conv_w_out.astype(BF16), vec(l1_conv_b_out), h,
                     vec(l1_ln1_g), vec(l1_ln1_b), S)

    w_router = jnp.concatenate(
        [l1_moe_w_router.astype(F32), jnp.zeros((D, LANES - N_EXPERTS), F32)], axis=1)
    route, count_rows = _router(h, w_router)
    counts = count_rows[0, :N_EXPERTS].astype(jnp.int32)
    starts = jnp.cumsum(counts) - counts
    e12 = route[:, R_E1:R_E2 + 1].astype(jnp.int32)
    rank12 = route[:, R_RANK1:R_RANK2 + 1].astype(jnp.int32)
    dest = (starts[e12] + rank12).reshape(-1)
    xs = _dispatch(h, dest)
    ys = _experts(xs, l1_moe_w_gu.astype(BF16), l1_moe_w_down.astype(BF16),
                  _work_list(counts, T * TOP_K // EXP_TM))
    out = _combine(dest, ys, route, h, vec(l1_ln2_g), vec(l1_ln2_b))
    return out.reshape(B, S, D)
```
